```python
import jax
import jax.numpy as jnp
from jax import lax
import numpy as np

D_MODEL = 2048
BATCH = 8
SEQ = 2048
DEPTH = 2
DEC_BATCH = 128
DEC_SEQ = 1
PAST_LEN = 2048
PAGE_SIZE = 128

N_MIX_LAYERS = (DEPTH + 1) // 2
N_ATTN_LAYERS = DEPTH // 2
A_WIDTH = D_MODEL // 2
A_GROUP = 128
A_GROUPS = A_WIDTH // A_GROUP
CHUNK = 128
B_WIDTH = D_MODEL - A_WIDTH
CONV_W = 3
MIX_SPLITS = (A_WIDTH, 2 * A_WIDTH, 2 * A_WIDTH + B_WIDTH, 2 * A_WIDTH + 2 * B_WIDTH)
MIX_IN = 2 * A_WIDTH + 3 * B_WIDTH
HEAD_DIM = 128
N_HEADS = D_MODEL // HEAD_DIM
N_KV_HEADS = 4
GQA_GROUP = N_HEADS // N_KV_HEADS
IDX_HEADS = 16
IDX_DIM = 64
TOPK_MAX = 256
QBLOCK = 128
Q_W = N_HEADS * HEAD_DIM
KV_W = N_KV_HEADS * HEAD_DIM
QI_W = IDX_HEADS * IDX_DIM
ATTN_SPLITS = (Q_W, Q_W + KV_W, Q_W + 2 * KV_W, Q_W + 2 * KV_W + QI_W, Q_W + 2 * KV_W + QI_W + IDX_DIM)
ATTN_IN = Q_W + 2 * KV_W + QI_W + IDX_DIM + IDX_HEADS
ATTN_SCALE = HEAD_DIM ** -0.5
IDX_SCALE = (IDX_DIM * IDX_HEADS) ** -0.5
MEM_LEN = 256
MEM_HEADS = 4
MEM_HEAD_DIM = 128
MEM_W = MEM_HEADS * MEM_HEAD_DIM
MEM_SCALE = MEM_HEAD_DIM ** -0.5
D_FF = 5632
LN_EPS = 1e-5
ALPHA = (2 * DEPTH) ** 0.25
BETA = (8 * DEPTH) ** -0.25

kernel_name = 'hybrid_sgu_conv_dsa_decoder_step'


def layer_norm(x, g, b):
    xf = x.astype(jnp.float32)
    xc = xf - jnp.mean(xf, axis=-1, keepdims=True)
    var = jnp.mean(xc * xc, axis=-1, keepdims=True)
    y = xc * lax.rsqrt(var + LN_EPS) * g.astype(jnp.float32) + b.astype(jnp.float32)
    return y.astype(x.dtype)


def post_ln(h, f, g, b):
    return layer_norm(ALPHA * h + f, g, b)


def swiglu(x, w_gu, w_down):
    gate, up = jnp.split(x @ w_gu, 2, axis=-1)
    return (jax.nn.silu(gate) * up) @ w_down


def gather_rows(rows, idx):
    return jax.vmap(lambda r, i: r[i])(rows, idx)


def sgu(u, vn, w_s, b_s):
    b, t = u.shape[:2]
    n = min(t, CHUNK)
    c = t // n
    mask = jnp.tril(jnp.ones((n, n), dtype=bool))
    w = jnp.where(mask[None], w_s[:, :n, :n], 0)
    vc = vn.reshape(b, c, n, A_GROUPS, A_GROUP)
    mixed = jnp.einsum('gts,bcsgd->bctgd', w, vc) + jnp.swapaxes(b_s[:, :n], 0, 1)[:, :, None]
    return (u.reshape(b, c, n, A_GROUPS, A_GROUP) * mixed).reshape(b, t, A_WIDTH)


def short_conv(xin, buf_prev, taps):
    t = xin.shape[1]
    buf = jnp.concatenate([buf_prev.astype(xin.dtype), xin], axis=1)
    y = taps[0] * buf[:, 0:t]
    for j in range(1, CONV_W):
        y = y + taps[j] * buf[:, j:j + t]
    return y, buf[:, t:]


def mix_layer(x, buf_prev, w_in, w_s, b_s, v_g, v_b, conv_w, w_out):
    b, t = x.shape[:2]
    u, v, gate_b, gate_c, h = jnp.split(x @ w_in, MIX_SPLITS, axis=-1)
    vn = layer_norm(v.reshape(b, t, A_GROUPS, A_GROUP), v_g, v_b)
    a_out = sgu(u, vn, w_s, b_s)
    if buf_prev is None:
        buf_prev = jnp.zeros((b, CONV_W - 1, B_WIDTH), x.dtype)
    conv_out, buf_new = short_conv(gate_c * h, buf_prev, conv_w)
    y = jnp.concatenate([a_out, gate_b * conv_out], axis=-1) @ w_out
    return y, buf_new, vn


def dsa_project(x, w_in):
    b, t = x.shape[:2]
    q, k, v, qi, ki, wi = jnp.split(x @ w_in, ATTN_SPLITS, axis=-1)
    return (q.reshape(b, t, N_KV_HEADS, GQA_GROUP, HEAD_DIM),
            k.reshape(b, t, N_KV_HEADS, HEAD_DIM),
            v.reshape(b, t, N_KV_HEADS, HEAD_DIM),
            qi.reshape(b, t, IDX_HEADS, IDX_DIM), ki, wi)


def index_scores(qi, wi, ki, qpos, kpos):
    s = jax.nn.relu(jnp.einsum('bthd,bsd->bths', qi, ki).astype(jnp.float32))
    sc = jnp.einsum('bths,bth->bts', s, wi.astype(jnp.float32)) * IDX_SCALE
    return jnp.where(kpos[None, None, :] <= qpos[None, :, None], sc, -jnp.inf)


def attend_selected(q, ksel, vsel, valid):
    b, t = q.shape[:2]
    s = jnp.einsum('btkgd,btnkd->btkgn', q, ksel).astype(jnp.float32) * ATTN_SCALE
    s = jnp.where(valid[:, :, None, None, :], s, -jnp.inf)
    p = jax.nn.softmax(s, axis=-1).astype(vsel.dtype)
    return jnp.einsum('btkgn,btnkd->btkgd', p, vsel).reshape(b, t, Q_W)


def dsa_prompt(x, w_in, w_out):
    b, s = x.shape[:2]
    q, k, v, qi, ki, wi = dsa_project(x, w_in)
    topk = min(TOPK_MAX, s // 4)
    kpos = jnp.arange(s)

    def block(i):
        t0 = i * QBLOCK
        sl = lambda a: lax.dynamic_slice_in_dim(a, t0, QBLOCK, axis=1)
        qpos = t0 + jnp.arange(QBLOCK)
        sc = index_scores(sl(qi), sl(wi), ki, qpos, kpos)
        _, idx = lax.top_k(sc, topk)
        valid = idx <= qpos[None, :, None]
        return attend_selected(sl(q), gather_rows(k, idx), gather_rows(v, idx), valid)

    o = lax.map(block, jnp.arange(s // QBLOCK))
    o = jnp.swapaxes(o, 0, 1).reshape(b, s, Q_W)
    return o @ w_out, k, v, ki


def dsa_sample(x, cache_k, cache_v, cache_ki, page_table, w_in, w_out):
    b, t = x.shape[:2]
    q, k, v, qi, ki, wi = dsa_project(x, w_in)
    past = page_table.shape[1] * PAGE_SIZE
    ki_past = cache_ki[page_table].reshape(b, past, IDX_DIM)
    ki_all = jnp.concatenate([ki_past.astype(ki.dtype), ki], axis=1)
    n_keys = past + t
    topk = min(TOPK_MAX, n_keys // 4)
    qpos = past + jnp.arange(t)
    sc = index_scores(qi, wi, ki_all, qpos, jnp.arange(n_keys))
    _, idx = lax.top_k(sc, topk)
    valid = idx <= qpos[None, :, None]
    in_past = (idx < past)[..., None, None]
    pidx = jnp.minimum(idx, past - 1)
    phys = gather_rows(page_table, pidx // PAGE_SIZE)
    off = pidx % PAGE_SIZE
    nidx = jnp.clip(idx - past, 0, t - 1)
    ksel = jnp.where(in_past, cache_k[phys, off].astype(k.dtype), gather_rows(k, nidx))
    vsel = jnp.where(in_past, cache_v[phys, off].astype(v.dtype), gather_rows(v, nidx))
    o = attend_selected(q, ksel, vsel, valid)
    return o @ w_out, k, v, ki


def mem_kv(mem, w_k, w_v):
    b, m = mem.shape[:2]
    return ((mem @ w_k).reshape(b, m, MEM_HEADS, MEM_HEAD_DIM),
            (mem @ w_v).reshape(b, m, MEM_HEADS, MEM_HEAD_DIM))


def mem_cross(x, mk, mv, w_q, w_o):
    b, t = x.shape[:2]
    q = (x @ w_q).reshape(b, t, MEM_HEADS, MEM_HEAD_DIM)
    s = jnp.einsum('bthd,bmhd->bhtm', q, mk.astype(q.dtype)).astype(jnp.float32) * MEM_SCALE
    p = jax.nn.softmax(s, axis=-1).astype(x.dtype)
    o = jnp.einsum('bhtm,bmhd->bthd', p, mv.astype(x.dtype)).reshape(b, t, MEM_W)
    return o @ w_o


def setup_inputs(seed: int = 0) -> dict:
    key = jax.random.key(seed)
    ks = jax.random.split(key, 32)
    f32 = jnp.float32

    def nrm(k, shape, scale=1.0):
        return jax.random.normal(k, shape, f32) * scale

    n_pages = PAST_LEN // PAGE_SIZE
    n_used = DEC_BATCH * n_pages
    n_pool = (5 * n_used + 3) // 4
    page_table = jax.random.permutation(ks[0], n_pool)[:n_used].reshape(DEC_BATCH, n_pages).astype(jnp.int32)
    dsc = D_MODEL ** -0.5
    return {
        'x_prompt': nrm(ks[1], (BATCH, SEQ, D_MODEL)),
        'x_sample': nrm(ks[2], (DEC_BATCH, DEC_SEQ, D_MODEL)),
        'cache_attn_k': nrm(ks[3], (N_ATTN_LAYERS, n_pool, PAGE_SIZE, N_KV_HEADS, HEAD_DIM)),
        'cache_attn_v': nrm(ks[4], (N_ATTN_LAYERS, n_pool, PAGE_SIZE, N_KV_HEADS, HEAD_DIM)),
        'cache_idx_k': nrm(ks[5], (N_ATTN_LAYERS, n_pool, PAGE_SIZE, IDX_DIM)),
        'state_conv': nrm(ks[6], (N_MIX_LAYERS, DEC_BATCH, CONV_W - 1, B_WIDTH)),
        'cache_mem_k': nrm(ks[7], (DEPTH, DEC_BATCH, MEM_LEN, MEM_HEADS, MEM_HEAD_DIM)),
        'cache_mem_v': nrm(ks[8], (DEPTH, DEC_BATCH, MEM_LEN, MEM_HEADS, MEM_HEAD_DIM)),
        'page_table': page_table,
        'mem_prompt': nrm(ks[9], (BATCH, MEM_LEN, D_MODEL)),
        'w_in_mix': nrm(ks[10], (N_MIX_LAYERS, D_MODEL, MIX_IN), dsc),
        'sgu_w': nrm(ks[11], (N_MIX_LAYERS, A_GROUPS, CHUNK, CHUNK), CHUNK ** -0.5),
        'sgu_b': 1.0 + nrm(ks[12], (N_MIX_LAYERS, A_GROUPS, CHUNK), 0.01),
        'sgu_ln_g': 1.0 + nrm(ks[13], (N_MIX_LAYERS, A_GROUPS, A_GROUP), 0.02),
        'sgu_ln_b': nrm(ks[14], (N_MIX_LAYERS, A_GROUPS, A_GROUP), 0.02),
        'conv_w': nrm(ks[15], (N_MIX_LAYERS, CONV_W, B_WIDTH), CONV_W ** -0.5),
        'w_out_mix': nrm(ks[16], (N_MIX_LAYERS, D_MODEL, D_MODEL), dsc * BETA),
        'w_in_attn': nrm(ks[17], (N_ATTN_LAYERS, D_MODEL, ATTN_IN), dsc),
        'w_out_attn': nrm(ks[18], (N_ATTN_LAYERS, Q_W, D_MODEL), Q_W ** -0.5 * BETA),
        'w_mem_q': nrm(ks[19], (DEPTH, D_MODEL, MEM_W), dsc),
        'w_mem_k': nrm(ks[20], (DEPTH, D_MODEL, MEM_W), dsc),
        'w_mem_v': nrm(ks[21], (DEPTH, D_MODEL, MEM_W), dsc),
        'w_mem_o': nrm(ks[22], (DEPTH, MEM_W, D_MODEL), MEM_W ** -0.5 * BETA),
        'ffn_w_gu': nrm(ks[23], (DEPTH, 2, D_MODEL, 2 * D_FF), dsc),
        'ffn_w_down': nrm(ks[24], (DEPTH, 2, D_FF, D_MODEL), D_FF ** -0.5 * BETA),
        'ln_g': 1.0 + nrm(ks[25], (DEPTH, 4, D_MODEL), 0.02),
        'ln_b': nrm(ks[26], (DEPTH, 4, D_MODEL), 0.02),
    }


def reference(x_prompt, x_sample, cache_attn_k, cache_attn_v, cache_idx_k, state_conv, cache_mem_k, cache_mem_v,
              page_table, mem_prompt, w_in_mix, sgu_w, sgu_b, sgu_ln_g, sgu_ln_b, conv_w, w_out_mix,
              w_in_attn, w_out_attn, w_mem_q, w_mem_k, w_mem_v, w_mem_o, ffn_w_gu, ffn_w_down, ln_g, ln_b):
    hp = x_prompt
    hs = x_sample
    attn_k_p, attn_v_p, idx_k_p, conv_p, mem_k_p, mem_v_p = [], [], [], [], [], []
    attn_k_s, attn_v_s, idx_k_s, conv_s, sgu_v_s = [], [], [], [], []
    for l in range(DEPTH):
        j = l // 2
        hp = post_ln(hp, 0.5 * swiglu(hp, ffn_w_gu[l, 0], ffn_w_down[l, 0]), ln_g[l, 0], ln_b[l, 0])
        hs = post_ln(hs, 0.5 * swiglu(hs, ffn_w_gu[l, 0], ffn_w_down[l, 0]), ln_g[l, 0], ln_b[l, 0])
        if l % 2 == 0:
            mix_w = (w_in_mix[j], sgu_w[j], sgu_b[j], sgu_ln_g[j], sgu_ln_b[j], conv_w[j], w_out_mix[j])
            yp, cbuf_p, _ = mix_layer(hp, None, *mix_w)
            ys, cbuf_s, vn_s = mix_layer(hs, state_conv[j], *mix_w)
            conv_p.append(cbuf_p)
            conv_s.append(cbuf_s)
            sgu_v_s.append(vn_s)
        else:
            yp, kp, vp, kip = dsa_prompt(hp, w_in_attn[j], w_out_attn[j])
            ys, ksn, vsn, kisn = dsa_sample(hs, cache_attn_k[j], cache_attn_v[j], cache_idx_k[j], page_table,
                                            w_in_attn[j], w_out_attn[j])
            attn_k_p.append(kp)
            attn_v_p.append(vp)
            idx_k_p.append(kip)
            attn_k_s.append(ksn)
            attn_v_s.append(vsn)
            idx_k_s.append(kisn)
        hp = post_ln(hp, yp, ln_g[l, 1], ln_b[l, 1])
        hs = post_ln(hs, ys, ln_g[l, 1], ln_b[l, 1])
        mkp, mvp = mem_kv(mem_prompt, w_mem_k[l], w_mem_v[l])
        mem_k_p.append(mkp)
        mem_v_p.append(mvp)
        hp = post_ln(hp, mem_cross(hp, mkp, mvp, w_mem_q[l], w_mem_o[l]), ln_g[l, 2], ln_b[l, 2])
        hs = post_ln(hs, mem_cross(hs, cache_mem_k[l], cache_mem_v[l], w_mem_q[l], w_mem_o[l]), ln_g[l, 2], ln_b[l, 2])
        hp = post_ln(hp, 0.5 * swiglu(hp, ffn_w_gu[l, 1], ffn_w_down[l, 1]), ln_g[l, 3], ln_b[l, 3])
        hs = post_ln(hs, 0.5 * swiglu(hs, ffn_w_gu[l, 1], ffn_w_down[l, 1]), ln_g[l, 3], ln_b[l, 3])
    y_prompt = hp
    y_sample = hs
    new_attn_k_prompt = jnp.stack(attn_k_p)
    new_attn_v_prompt = jnp.stack(attn_v_p)
    new_idx_k_prompt = jnp.stack(idx_k_p)
    new_conv_prompt = jnp.stack(conv_p)
    new_mem_k_prompt = jnp.stack(mem_k_p)
    new_mem_v_prompt = jnp.stack(mem_v_p)
    new_attn_k_sample = jnp.stack(attn_k_s)
    new_attn_v_sample = jnp.stack(attn_v_s)
    new_idx_k_sample = jnp.stack(idx_k_s)
    new_conv_sample = jnp.stack(conv_s)
    new_sgu_v_sample = jnp.stack(sgu_v_s)
    return (y_prompt, y_sample, new_attn_k_prompt, new_attn_v_prompt, new_idx_k_prompt, new_conv_prompt,
            new_mem_k_prompt, new_mem_v_prompt, new_attn_k_sample, new_attn_v_sample, new_idx_k_sample,
            new_conv_sample, new_sgu_v_sample)
```

```python
import functools

import jax
import jax.numpy as jnp
from jax import lax
from jax.experimental import pallas as pl
from jax.experimental.pallas import tpu as pltpu

F32 = jnp.float32
BF16 = jnp.bfloat16

DEPTH = 2
A_GROUP = 128
CHUNK = 128
CONV_W = 3
HEAD_DIM = 128
N_KV_HEADS = 4
GQA_GROUP = 4
IDX_HEADS = 16
IDX_DIM = 64
TOPK_MAX = 256
PAGE_SIZE = 128
MEM_HEADS = 4
MEM_HEAD_DIM = 128
LN_EPS = 1e-5
ALPHA = (2 * DEPTH) ** 0.25
ATTN_SCALE = HEAD_DIM ** -0.5
IDX_SCALE = (IDX_DIM * IDX_HEADS) ** -0.5
MEM_SCALE = MEM_HEAD_DIM ** -0.5
QBLOCK_ROWS = 128

LANES = 128
SUBLANES = 8
VMEM_LIMIT_BYTES = 56 * 1024 * 1024

INT_MIN = -(2 ** 31)


def _params(*semantics):
    return pltpu.CompilerParams(dimension_semantics=semantics, vmem_limit_bytes=VMEM_LIMIT_BYTES)


def _layer_norm(z, g, b):
    mu = jnp.mean(z, axis=-1, keepdims=True)
    zc = z - mu
    var = jnp.mean(zc * zc, axis=-1, keepdims=True)
    return zc * lax.rsqrt(var + LN_EPS) * g + b


def _dot(a, b):
    return jnp.dot(a, b, preferred_element_type=F32)


def _dot_nt(a, b):
    return lax.dot_general(a, b, (((1,), (1,)), ((), ())), preferred_element_type=F32)


def _ffn_ln_body(x_ref, wg_ref, wu_ref, wd_ref, g_ref, b_ref, o_ref, xb_ref, acc_ref):
    j = pl.program_id(1)

    @pl.when(j == 0)
    def _():
        xb_ref[...] = x_ref[...].astype(BF16)
        acc_ref[...] = jnp.zeros_like(acc_ref)

    xb = xb_ref[...]
    gate = _dot(xb, wg_ref[...])
    up = _dot(xb, wu_ref[...])
    act = (gate * jax.nn.sigmoid(gate)) * up
    acc_ref[...] += _dot(act.astype(BF16), wd_ref[...])

    @pl.when(j == pl.num_programs(1) - 1)
    def _():
        z = ALPHA * x_ref[...] + 0.5 * acc_ref[...]
        o_ref[...] = _layer_norm(z, g_ref[...], b_ref[...])


def _ffn_ln(x, w_gu, w_down, g, b, *, tm, tf):
    m, d = x.shape
    f = w_down.shape[0]
    nf = f // tf
    return pl.pallas_call(
        _ffn_ln_body,
        grid=(m // tm, nf),
        in_specs=[
            pl.BlockSpec((tm, d), lambda i, j: (i, 0)),
            pl.BlockSpec((d, tf), lambda i, j: (0, j)),
            pl.BlockSpec((d, tf), lambda i, j: (0, j + nf)),
            pl.BlockSpec((tf, d), lambda i, j: (j, 0)),
            pl.BlockSpec((1, d), lambda i, j: (0, 0)),
            pl.BlockSpec((1, d), lambda i, j: (0, 0)),
        ],
        out_specs=pl.BlockSpec((tm, d), lambda i, j: (i, 0)),
        out_shape=jax.ShapeDtypeStruct((m, d), F32),
        scratch_shapes=[pltpu.VMEM((tm, d), BF16), pltpu.VMEM((tm, d), F32)],
        compiler_params=_params("parallel", "arbitrary"),
        name="ffn_ln",
    )(x, w_gu, w_gu, w_down, g, b)


def _mm_body(x_ref, w_ref, o_ref, xb_ref):
    @pl.when(pl.program_id(1) == 0)
    def _():
        xb_ref[...] = x_ref[...].astype(BF16)

    o_ref[...] = _dot(xb_ref[...], w_ref[...])


def _mm(x, w, *, tm, tn):
    m, k = x.shape
    n = w.shape[1]
    return pl.pallas_call(
        _mm_body,
        grid=(m // tm, n // tn),
        in_specs=[
            pl.BlockSpec((tm, k), lambda i, j: (i, 0)),
            pl.BlockSpec((k, tn), lambda i, j: (0, j)),
        ],
        out_specs=pl.BlockSpec((tm, tn), lambda i, j: (i, j)),
        out_shape=jax.ShapeDtypeStruct((m, n), F32),
        scratch_shapes=[pltpu.VMEM((tm, k), BF16)],
        compiler_params=_params("parallel", "arbitrary"),
        name="proj",
    )(x, w)


def _mm_res_ln_body(a_ref, w_ref, h_ref, g_ref, b_ref, o_ref):
    y = _dot(a_ref[...].astype(BF16), w_ref[...])
    o_ref[...] = _layer_norm(ALPHA * h_ref[...] + y, g_ref[...], b_ref[...])


def _mm_res_ln(a, w, h, g, b, *, tm):
    m, k = a.shape
    d = w.shape[1]
    return pl.pallas_call(
        _mm_res_ln_body,
        grid=(m // tm,),
        in_specs=[
            pl.BlockSpec((tm, k), lambda i: (i, 0)),
            pl.BlockSpec((k, d), lambda i: (0, 0)),
            pl.BlockSpec((tm, d), lambda i: (i, 0)),
            pl.BlockSpec((1, d), lambda i: (0, 0)),
            pl.BlockSpec((1, d), lambda i: (0, 0)),
        ],
        out_specs=pl.BlockSpec((tm, d), lambda i: (i, 0)),
        out_shape=jax.ShapeDtypeStruct((m, d), F32),
        compiler_params=_params("parallel"),
        name="out_proj_ln",
    )(a, w, h, g, b)


def _mix_prompt_body(u_ref, v_ref, gb_ref, gc_ref, hh_ref, gcp_ref, hhp_ref, w_ref, bst_ref, vg_ref, vb_ref,
                     taps_ref, z_ref, ctail_ref, cin_ref, *, tt, aw):
    i = pl.program_id(1)
    cin = gc_ref[...] * hh_ref[...]
    cin_ref[0:SUBLANES, :] = jnp.where(i > 0, gcp_ref[...] * hhp_ref[...], 0.0)
    cin_ref[SUBLANES:SUBLANES + tt, :] = cin
    taps = taps_ref[...]
    conv = taps[0:1, :] * cin_ref[SUBLANES - 2:SUBLANES - 2 + tt, :]
    conv = conv + taps[1:2, :] * cin_ref[SUBLANES - 1:SUBLANES - 1 + tt, :]
    conv = conv + taps[2:3, :] * cin
    z_ref[:, aw:] = (gb_ref[...] * conv).astype(z_ref.dtype)
    ctail_ref[0] = cin_ref[tt:tt + SUBLANES, :]

    nc = tt // CHUNK
    row = lax.broadcasted_iota(jnp.int32, (CHUNK, CHUNK), 0)
    col = lax.broadcasted_iota(jnp.int32, (CHUNK, CHUNK), 1)
    causal = col <= row
    for g in range(aw // A_GROUP):
        cs = slice(g * A_GROUP, (g + 1) * A_GROUP)
        wg = jnp.where(causal, w_ref[g], 0.0).astype(BF16)
        vn = [_layer_norm(v_ref[c * CHUNK:(c + 1) * CHUNK, cs], vg_ref[g:g + 1, :], vb_ref[g:g + 1, :])
              for c in range(nc)]
        mixed = _dot(wg, jnp.concatenate(vn, axis=1).astype(BF16)) + bst_ref[:, g:g + 1]
        for c in range(nc):
            rs = slice(c * CHUNK, (c + 1) * CHUNK)
            z_ref[rs, cs] = (u_ref[rs, cs] * mixed[:, c * CHUNK:(c + 1) * CHUNK]).astype(z_ref.dtype)


def _mix_prompt(p, sgu_w, sgu_b_t, v_g, v_b, taps, *, nb, t, tt):
    aw = v_g.shape[0] * A_GROUP
    bw = taps.shape[1]
    nt = t // tt
    halo = tt // SUBLANES

    def rows(b, i):
        return b * nt + i

    def prev_rows(b, i):
        return jnp.maximum((b * nt + i) * halo - 1, 0)

    z, ctail = pl.pallas_call(
        functools.partial(_mix_prompt_body, tt=tt, aw=aw),
        grid=(nb, nt),
        in_specs=[
            pl.BlockSpec((tt, aw), lambda b, i: (rows(b, i), 0)),
            pl.BlockSpec((tt, aw), lambda b, i: (rows(b, i), 1)),
            pl.BlockSpec((tt, bw), lambda b, i: (rows(b, i), 2)),
            pl.BlockSpec((tt, bw), lambda b, i: (rows(b, i), 3)),
            pl.BlockSpec((tt, bw), lambda b, i: (rows(b, i), 4)),
            pl.BlockSpec((SUBLANES, bw), lambda b, i: (prev_rows(b, i), 3)),
            pl.BlockSpec((SUBLANES, bw), lambda b, i: (prev_rows(b, i), 4)),
            pl.BlockSpec(sgu_w.shape, lambda b, i: (0, 0, 0)),
            pl.BlockSpec(sgu_b_t.shape, lambda b, i: (0, 0)),
            pl.BlockSpec(v_g.shape, lambda b, i: (0, 0)),
            pl.BlockSpec(v_b.shape, lambda b, i: (0, 0)),
            pl.BlockSpec(taps.shape, lambda b, i: (0, 0)),
        ],
        out_specs=[
            pl.BlockSpec((tt, aw + bw), lambda b, i: (rows(b, i), 0)),
            pl.BlockSpec((1, SUBLANES, bw), lambda b, i: (b, 0, 0)),
        ],
        out_shape=[
            jax.ShapeDtypeStruct((nb * t, aw + bw), BF16),
            jax.ShapeDtypeStruct((nb, SUBLANES, bw), F32),
        ],
        scratch_shapes=[pltpu.VMEM((tt + SUBLANES, bw), F32)],
        compiler_params=_params("parallel", "arbitrary"),
        name="mix_prompt",
    )(p, p, p, p, p, p, p, sgu_w, sgu_b_t, v_g, v_b, taps)
    return z, ctail


def _mix_sample_body(u_ref, v_ref, gb_ref, gc_ref, hh_ref, s0_ref, s1_ref, w0_ref, b0_ref, vg_ref, vb_ref,
                     taps_ref, z_ref, vn_ref, cin_ref, *, aw):
    for g in range(aw // A_GROUP):
        cs = slice(g * A_GROUP, (g + 1) * A_GROUP)
        vn_ref[:, cs] = _layer_norm(v_ref[:, cs], vg_ref[:, cs], vb_ref[:, cs])
    mixed = w0_ref[...] * vn_ref[...] + b0_ref[...]
    z_ref[:, :aw] = (u_ref[...] * mixed).astype(z_ref.dtype)
    cin = gc_ref[...] * hh_ref[...]
    cin_ref[...] = cin
    taps = taps_ref[...]
    conv = taps[0:1, :] * s0_ref[...]
    conv = conv + taps[1:2, :] * s1_ref[...]
    conv = conv + taps[2:3, :] * cin
    z_ref[:, aw:] = (gb_ref[...] * conv).astype(z_ref.dtype)


def _mix_sample(p, s0, s1, w0, b0, vg, vb, taps):
    n = p.shape[0]
    aw = vg.shape[1]
    bw = taps.shape[1]

    def col(c, w):
        return pl.BlockSpec((n, w), lambda i, c=c: (0, c))

    def full(a):
        return pl.BlockSpec(a.shape, lambda i: (0,) * a.ndim)

    return pl.pallas_call(
        functools.partial(_mix_sample_body, aw=aw),
        grid=(1,),
        in_specs=[col(0, aw), col(1, aw), col(2, bw), col(3, bw), col(4, bw), full(s0), full(s1),
                  full(w0), full(b0), full(vg), full(vb), full(taps)],
        out_specs=[
            pl.BlockSpec((n, aw + bw), lambda i: (0, 0)),
            pl.BlockSpec((n, aw), lambda i: (0, 0)),
            pl.BlockSpec((n, bw), lambda i: (0, 0)),
        ],
        out_shape=[
            jax.ShapeDtypeStruct((n, aw + bw), BF16),
            jax.ShapeDtypeStruct((n, aw), F32),
            jax.ShapeDtypeStruct((n, bw), F32),
        ],
        compiler_params=_params("arbitrary"),
        name="mix_sample",
    )(p, p, p, p, p, s0, s1, w0, b0, vg, vb, taps)


def _select_topk(sc, adm, k, sel_ref):
    r, s = sc.shape
    bits = lax.bitcast_convert_type(sc, jnp.int32)
    key = jnp.where(bits >= 0, bits, jnp.int32(INT_MIN) - bits)
    key = jnp.where(adm, key, jnp.int32(INT_MIN))
    kf = float(k)

    def count(mask):
        return jnp.sum(jnp.where(mask, 1.0, 0.0), axis=1, keepdims=True)

    tau0 = jnp.where(count(key >= 0) >= kf, jnp.int32(0), jnp.int32(INT_MIN))

    def step(i, tau):
        cand = tau | jnp.left_shift(jnp.int32(1), 30 - i)
        return jnp.where(count(key >= cand) >= kf, cand, tau)

    tau = lax.fori_loop(0, 31, step, tau0)
    above = key > tau
    tied = jnp.logical_and(key == tau, tau > jnp.int32(INT_MIN))
    need = kf - count(above)
    sel_ref[...] = jnp.where(jnp.logical_or(above, tied), 1.0, 0.0)
    overfull = jnp.max(jnp.where(count(tied) > need, 1.0, 0.0))

    @pl.when(overfull > 0.0)
    def _():
        ri = lax.broadcasted_iota(jnp.int32, (LANES, LANES), 0)
        ci = lax.broadcasted_iota(jnp.int32, (LANES, LANES), 1)
        upper = jnp.where(ri <= ci, 1.0, 0.0).astype(BF16)
        carry = jnp.zeros((r, 1), F32)
        for c in range(s // LANES):
            cs = slice(c * LANES, (c + 1) * LANES)
            t = jnp.where(tied[:, cs], 1.0, 0.0)
            rank = _dot(t.astype(BF16), upper) + carry
            keep = jnp.logical_or(above[:, cs], jnp.logical_and(tied[:, cs], rank <= need))
            sel_ref[:, cs] = jnp.where(keep, 1.0, 0.0)
            carry = carry + jnp.sum(t, axis=1, keepdims=True)


def _dsa_prompt_body(q_ref, k_ref, v_ref, qi_ref, kwq_ref, kw_ref, o_ref, kt_ref, vb_ref, kit_ref, sel_ref,
                     *, tq, topk):
    i = pl.program_id(1)
    s_len = k_ref.shape[0]

    @pl.when(i == 0)
    def _():
        for kh in range(N_KV_HEADS):
            cs = slice(kh * HEAD_DIM, (kh + 1) * HEAD_DIM)
            kt_ref[cs, :] = k_ref[:, cs].T.astype(BF16)
        vb_ref[...] = v_ref[...].astype(BF16)
        kw = kw_ref[...]
        lane = lax.broadcasted_iota(jnp.int32, kw.shape, 1)
        kdup = jnp.where(lane < IDX_DIM, kw, pltpu.roll(kw, IDX_DIM, 1))
        kit_ref[...] = kdup.T.astype(BF16)

    kwq = kwq_ref[...]
    kit = kit_ref[...]
    low = lax.broadcasted_iota(jnp.int32, (tq, LANES), 1) < IDX_DIM
    acc = jnp.zeros((tq, s_len), F32)
    for pair in range(IDX_HEADS // 2):
        qp = qi_ref[:, pair * LANES:(pair + 1) * LANES]
        for half in range(2):
            h = 2 * pair + half
            lhs = jnp.where(low if half == 0 else jnp.logical_not(low), qp, 0.0).astype(BF16)
            acc = acc + jnp.maximum(_dot(lhs, kit), 0.0) * kwq[:, IDX_DIM + h:IDX_DIM + h + 1]
    sc = acc * IDX_SCALE

    qpos = i * tq + lax.broadcasted_iota(jnp.int32, (tq, s_len), 0)
    kpos = lax.broadcasted_iota(jnp.int32, (tq, s_len), 1)
    _select_topk(sc, kpos <= qpos, topk, sel_ref)
    sel = sel_ref[...] > 0.5

    for kh in range(N_KV_HEADS):
        heads = [kh * GQA_GROUP + g for g in range(GQA_GROUP)]
        q4 = jnp.concatenate([q_ref[:, h * HEAD_DIM:(h + 1) * HEAD_DIM] for h in heads], axis=0).astype(BF16)
        s_all = _dot(q4, kt_ref[kh * HEAD_DIM:(kh + 1) * HEAD_DIM, :])
        ps, ls = [], []
        for g in range(GQA_GROUP):
            sg = jnp.where(sel, s_all[g * tq:(g + 1) * tq, :] * ATTN_SCALE, -jnp.inf)
            p = jnp.exp(sg - jnp.max(sg, axis=-1, keepdims=True))
            ls.append(jnp.sum(p, axis=-1, keepdims=True))
            ps.append(p.astype(BF16))
        o = _dot(jnp.concatenate(ps, axis=0), vb_ref[:, kh * HEAD_DIM:(kh + 1) * HEAD_DIM])
        for g, h in enumerate(heads):
            o_ref[:, h * HEAD_DIM:(h + 1) * HEAD_DIM] = (o[g * tq:(g + 1) * tq, :] / ls[g]).astype(o_ref.dtype)


def _dsa_prompt(p, *, nb, t, tq, topk):
    qw = N_KV_HEADS * GQA_GROUP * HEAD_DIM
    kvw = N_KV_HEADS * HEAD_DIM
    qiw = IDX_HEADS * IDX_DIM
    nq = t // tq
    k_blk, v_blk = qw // kvw, qw // kvw + 1
    qi_blk = (qw + 2 * kvw) // qiw
    kw_blk = (qw + 2 * kvw + qiw) // LANES
    assert qw % kvw == 0 and (qw + 2 * kvw) % qiw == 0 and (qw + 2 * kvw + qiw) % LANES == 0
    return pl.pallas_call(
        functools.partial(_dsa_prompt_body, tq=tq, topk=topk),
        grid=(nb, nq),
        in_specs=[
            pl.BlockSpec((tq, qw), lambda b, i: (b * nq + i, 0)),
            pl.BlockSpec((t, kvw), lambda b, i: (b, k_blk)),
            pl.BlockSpec((t, kvw), lambda b, i: (b, v_blk)),
            pl.BlockSpec((tq, qiw), lambda b, i: (b * nq + i, qi_blk)),
            pl.BlockSpec((tq, LANES), lambda b, i: (b * nq + i, kw_blk)),
            pl.BlockSpec((t, LANES), lambda b, i: (b, kw_blk)),
        ],
        out_specs=pl.BlockSpec((tq, qw), lambda b, i: (b * nq + i, 0)),
        out_shape=jax.ShapeDtypeStruct((nb * t, qw), BF16),
        scratch_shapes=[
            pltpu.VMEM((kvw, t), BF16),
            pltpu.VMEM((t, kvw), BF16),
            pltpu.VMEM((LANES, t), BF16),
            pltpu.VMEM((tq, t), F32),
        ],
        compiler_params=_params("parallel", "arbitrary"),
        name="dsa_prompt",
    )(p, p, p, p, p, p)


def _dsa_sample_scores_body(pt_ref, qi_ref, wi_ref, kin_ref, *rest, n_pages):
    del pt_ref
    page_refs, sc_ref = rest[:n_pages], rest[n_pages]
    qi = qi_ref[0].astype(BF16)
    new = jnp.broadcast_to(kin_ref[0], (PAGE_SIZE, IDX_DIM))
    keys = jnp.concatenate([r[0] for r in page_refs] + [new], axis=0).astype(BF16)
    s = jnp.maximum(_dot_nt(qi, keys), 0.0) * wi_ref[0]
    sc_ref[0] = jnp.sum(s, axis=0, keepdims=True) * IDX_SCALE


def _dsa_sample_scores(page_table, qi, wi, ki_new, cache_ki):
    n, n_pages = page_table.shape
    nkp = (n_pages + 1) * PAGE_SIZE
    page_specs = [pl.BlockSpec((1, PAGE_SIZE, IDX_DIM), lambda b, pt, pg=pg: (pt[b, pg], 0, 0))
                  for pg in range(n_pages)]
    return pl.pallas_call(
        functools.partial(_dsa_sample_scores_body, n_pages=n_pages),
        grid_spec=pltpu.PrefetchScalarGridSpec(
            num_scalar_prefetch=1,
            grid=(n,),
            in_specs=[
                pl.BlockSpec((1, IDX_HEADS, IDX_DIM), lambda b, pt: (b, 0, 0)),
                pl.BlockSpec((1, IDX_HEADS, 1), lambda b, pt: (b, 0, 0)),
                pl.BlockSpec((1, 1, IDX_DIM), lambda b, pt: (b, 0, 0)),
            ] + page_specs,
            out_specs=pl.BlockSpec((1, 1, nkp), lambda b, pt: (b, 0, 0)),
        ),
        out_shape=jax.ShapeDtypeStruct((n, 1, nkp), F32),
        compiler_params=_params("arbitrary"),
        name="dsa_sample_scores",
    )(page_table, qi, wi, ki_new, *([cache_ki] * n_pages))


def _dsa_sample_select_body(sc_ref, selt_ref, sel_ref, *, n_keys, topk):
    sc = sc_ref[...]
    kpos = lax.broadcasted_iota(jnp.int32, sc.shape, 1)
    _select_topk(sc, kpos < n_keys, topk, sel_ref)
    selt_ref[...] = sel_ref[...].T


def _dsa_sample_select(sc, *, n_keys, topk):
    n, nkp = sc.shape
    return pl.pallas_call(
        functools.partial(_dsa_sample_select_body, n_keys=n_keys, topk=topk),
        grid=(1,),
        in_specs=[pl.BlockSpec((n, nkp), lambda i: (0, 0))],
        out_specs=pl.BlockSpec((nkp, n), lambda i: (0, 0)),
        out_shape=jax.ShapeDtypeStruct((nkp, n), F32),
        scratch_shapes=[pltpu.VMEM((n, nkp), F32)],
        compiler_params=_params("arbitrary"),
        name="dsa_sample_select",
    )(sc)


def _head_indicator(n_groups, width):
    r = lax.broadcasted_iota(jnp.int32, (n_groups * width, LANES), 0)
    c = lax.broadcasted_iota(jnp.int32, (n_groups * width, LANES), 1)
    return jnp.where(r // width == c, 1.0, 0.0).astype(BF16)


def _head_expander(n_groups, width, first):
    r = lax.broadcasted_iota(jnp.int32, (LANES, n_groups * width), 0)
    c = lax.broadcasted_iota(jnp.int32, (LANES, n_groups * width), 1)
    return jnp.where(r == first + c // width, 1.0, 0.0).astype(BF16)


def _dsa_sample_attend_body(pt_ref, qt_ref, kn_ref, vn_ref, selt_ref, *rest, n_pages):
    del pt_ref
    k_refs, v_refs = rest[:n_pages], rest[n_pages:2 * n_pages]
    o_ref, s_ref = rest[2 * n_pages], rest[2 * n_pages + 1]
    b = pl.program_id(0)
    kvw = N_KV_HEADS * HEAD_DIM
    qt = qt_ref[0]
    ind = _head_indicator(GQA_GROUP * N_KV_HEADS, HEAD_DIM)
    for pg in range(n_pages + 1):
        kp = k_refs[pg][0] if pg < n_pages else jnp.broadcast_to(kn_ref[0], (PAGE_SIZE, kvw))
        e = jnp.concatenate([(kp * qt[g:g + 1, :]).astype(BF16) for g in range(GQA_GROUP)], axis=1)
        s_ref[pg * PAGE_SIZE:(pg + 1) * PAGE_SIZE, :] = _dot(e, ind)
    onehot = jnp.where(lax.broadcasted_iota(jnp.int32, (LANES, LANES), 0) == b, 1.0, 0.0).astype(BF16)
    selc = _dot(selt_ref[...].astype(BF16), onehot)
    s = jnp.where(selc > 0.5, s_ref[...] * ATTN_SCALE, -jnp.inf)
    p = jnp.exp(s - jnp.max(s, axis=0, keepdims=True))
    pn = (p / jnp.sum(p, axis=0, keepdims=True)).astype(BF16)
    outs = []
    for g in range(GQA_GROUP):
        expand = _head_expander(N_KV_HEADS, HEAD_DIM, g * N_KV_HEADS)
        acc = jnp.zeros((PAGE_SIZE, kvw), F32)
        for pg in range(n_pages + 1):
            vp = v_refs[pg][0] if pg < n_pages else jnp.broadcast_to(vn_ref[0], (PAGE_SIZE, kvw))
            acc = acc + _dot(pn[pg * PAGE_SIZE:(pg + 1) * PAGE_SIZE, :], expand) * vp
        outs.append(jnp.sum(acc, axis=0, keepdims=True))
    o_ref[0] = jnp.concatenate(outs, axis=0)


def _dsa_sample_attend(page_table, qt, k_new, v_new, selt, cache_k, cache_v):
    n, n_pages = page_table.shape
    kvw = N_KV_HEADS * HEAD_DIM
    nkp = (n_pages + 1) * PAGE_SIZE
    assert n == LANES, "the selection column is picked with a one-hot over the lane axis"
    page_specs = [pl.BlockSpec((1, PAGE_SIZE, kvw), lambda b, pt, pg=pg: (pt[b, pg], 0, 0))
                  for pg in range(n_pages)]
    return pl.pallas_call(
        functools.partial(_dsa_sample_attend_body, n_pages=n_pages),
        grid_spec=pltpu.PrefetchScalarGridSpec(
            num_scalar_prefetch=1,
            grid=(n,),
            in_specs=[
                pl.BlockSpec((1, GQA_GROUP, kvw), lambda b, pt: (b, 0, 0)),
                pl.BlockSpec((1, 1, kvw), lambda b, pt: (b, 0, 0)),
                pl.BlockSpec((1, 1, kvw), lambda b, pt: (b, 0, 0)),
                pl.BlockSpec((nkp, n), lambda b, pt: (0, 0)),
            ] + page_specs + page_specs,
            out_specs=pl.BlockSpec((1, GQA_GROUP, kvw), lambda b, pt: (b, 0, 0)),
            scratch_shapes=[pltpu.VMEM((nkp, LANES), F32)],
        ),
        out_shape=jax.ShapeDtypeStruct((n, GQA_GROUP, kvw), F32),
        compiler_params=_params("arbitrary"),
        name="dsa_sample_attend",
    )(page_table, qt, k_new, v_new, selt, *([cache_k] * n_pages), *([cache_v] * n_pages))


def _memx_prompt_body(q_ref, mk_ref, mv_ref, o_ref):
    for h in range(MEM_HEADS):
        cs = slice(h * MEM_HEAD_DIM, (h + 1) * MEM_HEAD_DIM)
        s = _dot_nt(q_ref[:, cs].astype(BF16), mk_ref[:, cs].astype(BF16)) * MEM_SCALE
        p = jnp.exp(s - jnp.max(s, axis=-1, keepdims=True))
        pn = (p / jnp.sum(p, axis=-1, keepdims=True)).astype(BF16)
        o_ref[:, cs] = _dot(pn, mv_ref[:, cs].astype(BF16)).astype(o_ref.dtype)


def _memx_prompt(q, mk, mv, *, nb, t, tq):
    w = q.shape[1]
    m_len = mk.shape[0] // nb
    nq = t // tq
    return pl.pallas_call(
        _memx_prompt_body,
        grid=(nb, nq),
        in_specs=[
            pl.BlockSpec((tq, w), lambda b, i: (b * nq + i, 0)),
            pl.BlockSpec((m_len, w), lambda b, i: (b, 0)),
            pl.BlockSpec((m_len, w), lambda b, i: (b, 0)),
        ],
        out_specs=pl.BlockSpec((tq, w), lambda b, i: (b * nq + i, 0)),
        out_shape=jax.ShapeDtypeStruct((nb * t, w), BF16),
        compiler_params=_params("parallel", "arbitrary"),
        name="memx_prompt",
    )(q, mk, mv)


def _memx_sample_body(q_ref, mk_ref, mv_ref, o_ref, *, ns):
    ind = _head_indicator(MEM_HEADS, MEM_HEAD_DIM)
    expand = _head_expander(MEM_HEADS, MEM_HEAD_DIM, 0)
    outs = []
    for i in range(ns):
        e = (mk_ref[i] * q_ref[i:i + 1, :]).astype(BF16)
        s = _dot(e, ind) * MEM_SCALE
        p = jnp.exp(s - jnp.max(s, axis=0, keepdims=True))
        pn = (p / jnp.sum(p, axis=0, keepdims=True)).astype(BF16)
        outs.append(jnp.sum(_dot(pn, expand) * mv_ref[i], axis=0, keepdims=True))
    o_ref[...] = jnp.concatenate(outs, axis=0).astype(o_ref.dtype)


def _memx_sample(q, mk, mv, *, ns):
    n, w = q.shape
    m_len = mk.shape[1]
    return pl.pallas_call(
        functools.partial(_memx_sample_body, ns=ns),
        grid=(n // ns,),
        in_specs=[
            pl.BlockSpec((ns, w), lambda i: (i, 0)),
            pl.BlockSpec((ns, m_len, w), lambda i: (i, 0, 0)),
            pl.BlockSpec((ns, m_len, w), lambda i: (i, 0, 0)),
        ],
        out_specs=pl.BlockSpec((ns, w), lambda i: (i, 0)),
        out_shape=jax.ShapeDtypeStruct((n, w), BF16),
        compiler_params=_params("parallel"),
        name="memx_sample",
    )(q, mk, mv)


def _row_tile(m, cap):
    t = min(m, cap)
    assert m % t == 0
    return t


def _col_tile(n, cap):
    best = LANES
    for t in range(LANES, min(n, cap) + 1, LANES):
        if n % t == 0:
            best = t
    assert n % best == 0
    return best


def _proj(x, w):
    return _mm(x, w, tm=_row_tile(x.shape[0], 1024), tn=_col_tile(w.shape[1], 512))


def kernel(x_prompt, x_sample, cache_attn_k, cache_attn_v, cache_idx_k, state_conv, cache_mem_k, cache_mem_v, page_table, mem_prompt, w_in_mix, sgu_w, sgu_b, sgu_ln_g, sgu_ln_b, conv_w, w_out_mix, w_in_attn, w_out_attn, w_mem_q, w_mem_k, w_mem_v, w_mem_o, ffn_w_gu, ffn_w_down, ln_g, ln_b):
    nb, t, d = x_prompt.shape
    ns = x_sample.shape[0]
    assert x_sample.shape[1] == 1
    depth = ffn_w_gu.shape[0]
    d_ff = ffn_w_down.shape[2]
    m_len = mem_prompt.shape[1]
    mem_w = w_mem_q.shape[2]
    n_pages = page_table.shape[1]
    past = n_pages * PAGE_SIZE
    aw = sgu_ln_g.shape[1] * A_GROUP
    bw = conv_w.shape[2]
    qw = N_KV_HEADS * GQA_GROUP * HEAD_DIM
    kvw = N_KV_HEADS * HEAD_DIM
    qiw = IDX_HEADS * IDX_DIM

    hp = x_prompt.reshape(nb * t, d)
    hs = x_sample.reshape(ns, d)
    mem2 = mem_prompt.reshape(nb * m_len, d)
    tm_p = _row_tile(nb * t, 512)
    tf = _col_tile(d_ff, 512)

    def ln_params(l, i):
        return ln_g[l, i].reshape(1, d), ln_b[l, i].reshape(1, d)

    def ffn(h, l, i, tm):
        g, b = ln_params(l, 3 * i)
        return _ffn_ln(h, ffn_w_gu[l, i].astype(BF16), ffn_w_down[l, i].astype(BF16), g, b, tm=tm, tf=tf)

    attn_k_p, attn_v_p, idx_k_p, conv_p, mem_k_p, mem_v_p = [], [], [], [], [], []
    attn_k_s, attn_v_s, idx_k_s, conv_s, sgu_v_s = [], [], [], [], []
    for l in range(depth):
        j = l // 2
        hp = ffn(hp, l, 0, tm_p)
        hs = ffn(hs, l, 0, ns)

        g1, b1 = ln_params(l, 1)
        if l % 2 == 0:
            w_in = w_in_mix[j].astype(BF16)
            w_out = w_out_mix[j].astype(BF16)
            taps = conv_w[j]
            pp = _proj(hp, w_in)
            zp, ctail = _mix_prompt(pp, sgu_w[j], sgu_b[j].T, sgu_ln_g[j], sgu_ln_b[j], taps,
                                    nb=nb, t=t, tt=_row_tile(t, 256))
            conv_p.append(ctail[:, SUBLANES - (CONV_W - 1):, :])
            hp = _mm_res_ln(zp, w_out, hp, g1, b1, tm=tm_p)
            ps = _proj(hs, w_in)
            w0 = jnp.repeat(sgu_w[j][:, 0, 0], A_GROUP).reshape(1, aw)
            b0 = jnp.repeat(sgu_b[j][:, 0], A_GROUP).reshape(1, aw)
            zs, vn_s, cin_s = _mix_sample(ps, state_conv[j][:, 0, :], state_conv[j][:, 1, :], w0, b0,
                                          sgu_ln_g[j].reshape(1, aw), sgu_ln_b[j].reshape(1, aw), taps)
            conv_s.append(jnp.stack([state_conv[j][:, 1, :], cin_s], axis=1))
            sgu_v_s.append(vn_s.reshape(ns, 1, aw // A_GROUP, A_GROUP))
            hs = _mm_res_ln(zs, w_out, hs, g1, b1, tm=ns)
        else:
            n_in = w_in_attn.shape[2]
            n_pad = -n_in % LANES
            w_in = jnp.pad(w_in_attn[j], ((0, 0), (0, n_pad))).astype(BF16)
            w_out = w_out_attn[j].astype(BF16)
            pp = _proj(hp, w_in)
            op = _dsa_prompt(pp, nb=nb, t=t, tq=QBLOCK_ROWS, topk=min(TOPK_MAX, t // 4))
            attn_k_p.append(pp[:, qw:qw + kvw].reshape(nb, t, N_KV_HEADS, HEAD_DIM))
            attn_v_p.append(pp[:, qw + kvw:qw + 2 * kvw].reshape(nb, t, N_KV_HEADS, HEAD_DIM))
            idx_k_p.append(pp[:, qw + 2 * kvw + qiw:qw + 2 * kvw + qiw + IDX_DIM].reshape(nb, t, IDX_DIM))
            hp = _mm_res_ln(op, w_out, hp, g1, b1, tm=tm_p)
            ps = _proj(hs, w_in)
            q_s = ps[:, :qw]
            k_s = ps[:, qw:qw + kvw]
            v_s = ps[:, qw + kvw:qw + 2 * kvw]
            qi_s = ps[:, qw + 2 * kvw:qw + 2 * kvw + qiw].reshape(ns, IDX_HEADS, IDX_DIM)
            ki_s = ps[:, qw + 2 * kvw + qiw:qw + 2 * kvw + qiw + IDX_DIM]
            wi_s = ps[:, qw + 2 * kvw + qiw + IDX_DIM:qw + 2 * kvw + qiw + IDX_DIM + IDX_HEADS]
            n_pool = cache_idx_k.shape[1]
            sc = _dsa_sample_scores(page_table, qi_s, wi_s.reshape(ns, IDX_HEADS, 1), ki_s.reshape(ns, 1, IDX_DIM),
                                    cache_idx_k[j])
            selt = _dsa_sample_select(sc.reshape(ns, -1), n_keys=past + 1, topk=min(TOPK_MAX, (past + 1) // 4))
            qt = q_s.reshape(ns, N_KV_HEADS, GQA_GROUP, HEAD_DIM).transpose(0, 2, 1, 3).reshape(ns, GQA_GROUP, kvw)
            ot = _dsa_sample_attend(page_table, qt, k_s.reshape(ns, 1, kvw), v_s.reshape(ns, 1, kvw), selt,
                                    cache_attn_k[j].reshape(n_pool, PAGE_SIZE, kvw),
                                    cache_attn_v[j].reshape(n_pool, PAGE_SIZE, kvw))
            os_ = ot.reshape(ns, GQA_GROUP, N_KV_HEADS, HEAD_DIM).transpose(0, 2, 1, 3).reshape(ns, qw)
            attn_k_s.append(k_s.reshape(ns, 1, N_KV_HEADS, HEAD_DIM))
            attn_v_s.append(v_s.reshape(ns, 1, N_KV_HEADS, HEAD_DIM))
            idx_k_s.append(ki_s.reshape(ns, 1, IDX_DIM))
            hs = _mm_res_ln(os_, w_out, hs, g1, b1, tm=ns)

        g2, b2 = ln_params(l, 2)
        w_q = w_mem_q[l].astype(BF16)
        w_o = w_mem_o[l].astype(BF16)
        mk = _proj(mem2, w_mem_k[l].astype(BF16))
        mv = _proj(mem2, w_mem_v[l].astype(BF16))
        mem_k_p.append(mk.reshape(nb, m_len, MEM_HEADS, MEM_HEAD_DIM))
        mem_v_p.append(mv.reshape(nb, m_len, MEM_HEADS, MEM_HEAD_DIM))
        op = _memx_prompt(_proj(hp, w_q), mk, mv, nb=nb, t=t, tq=_row_tile(t, 512))
        hp = _mm_res_ln(op, w_o, hp, g2, b2, tm=tm_p)
        os_ = _memx_sample(_proj(hs, w_q), cache_mem_k[l].reshape(ns, m_len, mem_w),
                           cache_mem_v[l].reshape(ns, m_len, mem_w), ns=_row_tile(ns, SUBLANES))
        hs = _mm_res_ln(os_, w_o, hs, g2, b2, tm=ns)

        hp = ffn(hp, l, 1, tm_p)
        hs = ffn(hs, l, 1, ns)

    return (hp.reshape(nb, t, d), hs.reshape(ns, 1, d),
            jnp.stack(attn_k_p), jnp.stack(attn_v_p), jnp.stack(idx_k_p), jnp.stack(conv_p),
            jnp.stack(mem_k_p), jnp.stack(mem_v_p),
            jnp.stack(attn_k_s), jnp.stack(attn_v_s), jnp.stack(idx_k_s), jnp.stack(conv_s), jnp.stack(sgu_v_s))
```

```python
import functools

import jax
import jax.numpy as jnp
from jax import lax
from jax.experimental import pallas as pl
from jax.experimental.pallas import tpu as pltpu

F32 = jnp.float32
BF16 = jnp.bfloat16

DEPTH = 2
A_GROUP = 128
CHUNK = 128
CONV_W = 3
HEAD_DIM = 128
N_KV_HEADS = 4
GQA_GROUP = 4
IDX_HEADS = 16
IDX_DIM = 64
TOPK_MAX = 256
PAGE_SIZE = 128
MEM_HEADS = 4
MEM_HEAD_DIM = 128
LN_EPS = 1e-5
ALPHA = (2 * DEPTH) ** 0.25
ATTN_SCALE = HEAD_DIM ** -0.5
IDX_SCALE = (IDX_DIM * IDX_HEADS) ** -0.5
MEM_SCALE = MEM_HEAD_DIM ** -0.5
QBLOCK_ROWS = 128
DSA_KEY_EXTENTS = 4

LANES = 128
SUBLANES = 8
VMEM_LIMIT_BYTES = 56 * 1024 * 1024

INT_MIN = -(2 ** 31)


def _params(*semantics):
    return pltpu.CompilerParams(dimension_semantics=semantics, vmem_limit_bytes=VMEM_LIMIT_BYTES)


def _layer_norm(z, g, b):
    mu = jnp.mean(z, axis=-1, keepdims=True)
    zc = z - mu
    var = jnp.mean(zc * zc, axis=-1, keepdims=True)
    return zc * lax.rsqrt(var + LN_EPS) * g + b


def _dot(a, b):
    return jnp.dot(a, b, preferred_element_type=F32)


def _dot_nt(a, b):
    return lax.dot_general(a, b, (((1,), (1,)), ((), ())), preferred_element_type=F32)


def _dot_tn(a, b):
    return lax.dot_general(a, b, (((0,), (0,)), ((), ())), preferred_element_type=F32)


def _ffn_ln_body(x_ref, wg_ref, wu_ref, wd_ref, g_ref, b_ref, o_ref, xb_ref, acc_ref):
    j = pl.program_id(1)

    @pl.when(j == 0)
    def _():
        xb_ref[...] = x_ref[...].astype(BF16)
        acc_ref[...] = jnp.zeros_like(acc_ref)

    xb = xb_ref[...]
    gate = _dot(xb, wg_ref[...])
    up = _dot(xb, wu_ref[...])
    act = (gate * jax.nn.sigmoid(gate)) * up
    acc_ref[...] += _dot(act.astype(BF16), wd_ref[...])

    @pl.when(j == pl.num_programs(1) - 1)
    def _():
        z = ALPHA * x_ref[...] + 0.5 * acc_ref[...]
        o_ref[...] = _layer_norm(z, g_ref[...], b_ref[...])


def _ffn_ln(x, w_gu, w_down, g, b, *, layer, slot, tm, tf):
    m, d = x.shape
    f = w_down.shape[2]
    nf = f // tf
    return pl.pallas_call(
        _ffn_ln_body,
        grid=(m // tm, nf),
        in_specs=[
            pl.BlockSpec((tm, d), lambda i, j: (i, 0)),
            pl.BlockSpec((None, None, d, tf), lambda i, j: (layer, slot, 0, j)),
            pl.BlockSpec((None, None, d, tf), lambda i, j: (layer, slot, 0, j + nf)),
            pl.BlockSpec((None, None, tf, d), lambda i, j: (layer, slot, j, 0)),
            pl.BlockSpec((1, d), lambda i, j: (0, 0)),
            pl.BlockSpec((1, d), lambda i, j: (0, 0)),
        ],
        out_specs=pl.BlockSpec((tm, d), lambda i, j: (i, 0)),
        out_shape=jax.ShapeDtypeStruct((m, d), F32),
        scratch_shapes=[pltpu.VMEM((tm, d), BF16), pltpu.VMEM((tm, d), F32)],
        compiler_params=_params("parallel", "arbitrary"),
        name="ffn_ln",
    )(x, w_gu, w_gu, w_down, g, b)


def _mm_body(x_ref, w_ref, o_ref, xb_ref):
    @pl.when(pl.program_id(1) == 0)
    def _():
        xb_ref[...] = x_ref[...].astype(BF16)

    o_ref[...] = _dot(xb_ref[...], w_ref[...])


def _mm(x, w, *, tm, tn):
    m, k = x.shape
    n = w.shape[1]
    return pl.pallas_call(
        _mm_body,
        grid=(m // tm, n // tn),
        in_specs=[
            pl.BlockSpec((tm, k), lambda i, j: (i, 0)),
            pl.BlockSpec((k, tn), lambda i, j: (0, j)),
        ],
        out_specs=pl.BlockSpec((tm, tn), lambda i, j: (i, j)),
        out_shape=jax.ShapeDtypeStruct((m, n), F32),
        scratch_shapes=[pltpu.VMEM((tm, k), BF16)],
        compiler_params=_params("parallel", "arbitrary"),
        name="proj",
    )(x, w)


def _mm_res_ln_body(a_ref, w_ref, h_ref, g_ref, b_ref, o_ref):
    y = _dot(a_ref[...].astype(BF16), w_ref[...])
    o_ref[...] = _layer_norm(ALPHA * h_ref[...] + y, g_ref[...], b_ref[...])


def _mm_res_ln(a, w, h, g, b, *, tm):
    m, k = a.shape
    d = w.shape[1]
    return pl.pallas_call(
        _mm_res_ln_body,
        grid=(m // tm,),
        in_specs=[
            pl.BlockSpec((tm, k), lambda i: (i, 0)),
            pl.BlockSpec((k, d), lambda i: (0, 0)),
            pl.BlockSpec((tm, d), lambda i: (i, 0)),
            pl.BlockSpec((1, d), lambda i: (0, 0)),
            pl.BlockSpec((1, d), lambda i: (0, 0)),
        ],
        out_specs=pl.BlockSpec((tm, d), lambda i: (i, 0)),
        out_shape=jax.ShapeDtypeStruct((m, d), F32),
        compiler_params=_params("parallel"),
        name="out_proj_ln",
    )(a, w, h, g, b)


def _mix_prompt_body(u_ref, v_ref, gb_ref, gc_ref, hh_ref, gcp_ref, hhp_ref, w_ref, bst_ref, vg_ref, vb_ref,
                     taps_ref, z_ref, ctail_ref, cin_ref, *, tt, aw):
    i = pl.program_id(1)
    cin = gc_ref[...] * hh_ref[...]
    cin_ref[0:SUBLANES, :] = jnp.where(i > 0, gcp_ref[...] * hhp_ref[...], 0.0)
    cin_ref[SUBLANES:SUBLANES + tt, :] = cin
    taps = taps_ref[...]
    conv = taps[0:1, :] * cin_ref[SUBLANES - 2:SUBLANES - 2 + tt, :]
    conv = conv + taps[1:2, :] * cin_ref[SUBLANES - 1:SUBLANES - 1 + tt, :]
    conv = conv + taps[2:3, :] * cin
    z_ref[:, aw:] = (gb_ref[...] * conv).astype(z_ref.dtype)
    ctail_ref[0] = cin_ref[tt:tt + SUBLANES, :]

    nc = tt // CHUNK
    row = lax.broadcasted_iota(jnp.int32, (CHUNK, CHUNK), 0)
    col = lax.broadcasted_iota(jnp.int32, (CHUNK, CHUNK), 1)
    causal = col <= row
    for g in range(aw // A_GROUP):
        cs = slice(g * A_GROUP, (g + 1) * A_GROUP)
        wg = jnp.where(causal, w_ref[g], 0.0).astype(BF16)
        vn = [_layer_norm(v_ref[c * CHUNK:(c + 1) * CHUNK, cs], vg_ref[g:g + 1, :], vb_ref[g:g + 1, :])
              for c in range(nc)]
        mixed = _dot(wg, jnp.concatenate(vn, axis=1).astype(BF16)) + bst_ref[:, g:g + 1]
        for c in range(nc):
            rs = slice(c * CHUNK, (c + 1) * CHUNK)
            z_ref[rs, cs] = (u_ref[rs, cs] * mixed[:, c * CHUNK:(c + 1) * CHUNK]).astype(z_ref.dtype)


def _mix_prompt(p, sgu_w, sgu_b_t, v_g, v_b, taps, *, nb, t, tt):
    aw = v_g.shape[0] * A_GROUP
    bw = taps.shape[1]
    nt = t // tt
    halo = tt // SUBLANES

    def rows(b, i):
        return b * nt + i

    def prev_rows(b, i):
        return jnp.maximum((b * nt + i) * halo - 1, 0)

    z, ctail = pl.pallas_call(
        functools.partial(_mix_prompt_body, tt=tt, aw=aw),
        grid=(nb, nt),
        in_specs=[
            pl.BlockSpec((tt, aw), lambda b, i: (rows(b, i), 0)),
            pl.BlockSpec((tt, aw), lambda b, i: (rows(b, i), 1)),
            pl.BlockSpec((tt, bw), lambda b, i: (rows(b, i), 2)),
            pl.BlockSpec((tt, bw), lambda b, i: (rows(b, i), 3)),
            pl.BlockSpec((tt, bw), lambda b, i: (rows(b, i), 4)),
            pl.BlockSpec((SUBLANES, bw), lambda b, i: (prev_rows(b, i), 3)),
            pl.BlockSpec((SUBLANES, bw), lambda b, i: (prev_rows(b, i), 4)),
            pl.BlockSpec(sgu_w.shape, lambda b, i: (0, 0, 0)),
            pl.BlockSpec(sgu_b_t.shape, lambda b, i: (0, 0)),
            pl.BlockSpec(v_g.shape, lambda b, i: (0, 0)),
            pl.BlockSpec(v_b.shape, lambda b, i: (0, 0)),
            pl.BlockSpec(taps.shape, lambda b, i: (0, 0)),
        ],
        out_specs=[
            pl.BlockSpec((tt, aw + bw), lambda b, i: (rows(b, i), 0)),
            pl.BlockSpec((1, SUBLANES, bw), lambda b, i: (b, 0, 0)),
        ],
        out_shape=[
            jax.ShapeDtypeStruct((nb * t, aw + bw), BF16),
            jax.ShapeDtypeStruct((nb, SUBLANES, bw), F32),
        ],
        scratch_shapes=[pltpu.VMEM((tt + SUBLANES, bw), F32)],
        compiler_params=_params("parallel", "arbitrary"),
        name="mix_prompt",
    )(p, p, p, p, p, p, p, sgu_w, sgu_b_t, v_g, v_b, taps)
    return z, ctail


def _mix_sample_body(u_ref, v_ref, gb_ref, gc_ref, hh_ref, s0_ref, s1_ref, w0_ref, b0_ref, vg_ref, vb_ref,
                     taps_ref, z_ref, vn_ref, cin_ref, *, aw):
    for g in range(aw // A_GROUP):
        cs = slice(g * A_GROUP, (g + 1) * A_GROUP)
        vn_ref[:, cs] = _layer_norm(v_ref[:, cs], vg_ref[:, cs], vb_ref[:, cs])
    mixed = w0_ref[...] * vn_ref[...] + b0_ref[...]
    z_ref[:, :aw] = (u_ref[...] * mixed).astype(z_ref.dtype)
    cin = gc_ref[...] * hh_ref[...]
    cin_ref[...] = cin
    taps = taps_ref[...]
    conv = taps[0:1, :] * s0_ref[...]
    conv = conv + taps[1:2, :] * s1_ref[...]
    conv = conv + taps[2:3, :] * cin
    z_ref[:, aw:] = (gb_ref[...] * conv).astype(z_ref.dtype)


def _mix_sample(p, s0, s1, w0, b0, vg, vb, taps):
    n = p.shape[0]
    aw = vg.shape[1]
    bw = taps.shape[1]

    def col(c, w):
        return pl.BlockSpec((n, w), lambda i, c=c: (0, c))

    def full(a):
        return pl.BlockSpec(a.shape, lambda i: (0,) * a.ndim)

    return pl.pallas_call(
        functools.partial(_mix_sample_body, aw=aw),
        grid=(1,),
        in_specs=[col(0, aw), col(1, aw), col(2, bw), col(3, bw), col(4, bw), full(s0), full(s1),
                  full(w0), full(b0), full(vg), full(vb), full(taps)],
        out_specs=[
            pl.BlockSpec((n, aw + bw), lambda i: (0, 0)),
            pl.BlockSpec((n, aw), lambda i: (0, 0)),
            pl.BlockSpec((n, bw), lambda i: (0, 0)),
        ],
        out_shape=[
            jax.ShapeDtypeStruct((n, aw + bw), BF16),
            jax.ShapeDtypeStruct((n, aw), F32),
            jax.ShapeDtypeStruct((n, bw), F32),
        ],
        compiler_params=_params("arbitrary"),
        name="mix_sample",
    )(p, p, p, p, p, s0, s1, w0, b0, vg, vb, taps)


def _select_topk(sc, adm, k, sel_ref):
    r, s = sc.shape
    bits = lax.bitcast_convert_type(sc, jnp.int32)
    key = jnp.where(bits >= 0, bits, jnp.int32(INT_MIN) - bits)
    key = jnp.where(adm, key, jnp.int32(INT_MIN))
    kf = float(k)

    def count(mask):
        return jnp.sum(jnp.where(mask, 1.0, 0.0), axis=1, keepdims=True)

    tau0 = jnp.where(count(key >= 0) >= kf, jnp.int32(0), jnp.int32(INT_MIN))

    def step(i, tau):
        cand = tau | jnp.left_shift(jnp.int32(1), 30 - i)
        return jnp.where(count(key >= cand) >= kf, cand, tau)

    tau = lax.fori_loop(0, 31, step, tau0)
    above = key > tau
    tied = jnp.logical_and(key == tau, tau > jnp.int32(INT_MIN))
    need = kf - count(above)
    sel_ref[...] = jnp.where(jnp.logical_or(above, tied), 1.0, 0.0)
    overfull = jnp.max(jnp.where(count(tied) > need, 1.0, 0.0))

    @pl.when(overfull > 0.0)
    def _():
        ri = lax.broadcasted_iota(jnp.int32, (LANES, LANES), 0)
        ci = lax.broadcasted_iota(jnp.int32, (LANES, LANES), 1)
        upper = jnp.where(ri <= ci, 1.0, 0.0).astype(BF16)
        carry = jnp.zeros((r, 1), F32)
        for c in range(s // LANES):
            cs = slice(c * LANES, (c + 1) * LANES)
            t = jnp.where(tied[:, cs], 1.0, 0.0)
            rank = _dot(t.astype(BF16), upper) + carry
            keep = jnp.logical_or(above[:, cs], jnp.logical_and(tied[:, cs], rank <= need))
            sel_ref[:, cs] = jnp.where(keep, 1.0, 0.0)
            carry = carry + jnp.sum(t, axis=1, keepdims=True)


def _dsa_prompt_body(q_ref, k_ref, v_ref, qi_ref, kwq_ref, kw_ref, o_ref, kt_ref, vb_ref, kit_ref, sel_ref,
                     *, tq, topk, n_ext):
    i = pl.program_id(1)
    s_len = k_ref.shape[0]

    @pl.when(i == 0)
    def _():
        for kh in range(N_KV_HEADS):
            cs = slice(kh * HEAD_DIM, (kh + 1) * HEAD_DIM)
            kt_ref[cs, :] = k_ref[:, cs].T.astype(BF16)
        vb_ref[...] = v_ref[...].astype(BF16)
        kw = kw_ref[...]
        lane = lax.broadcasted_iota(jnp.int32, kw.shape, 1)
        kdup = jnp.where(lane < IDX_DIM, kw, pltpu.roll(kw, IDX_DIM, 1))
        kit_ref[...] = kdup.T.astype(BF16)

    per = -(-s_len // (n_ext * tq))
    for e in range(n_ext):
        ext = min((e + 1) * per * tq, s_len)

        @pl.when(jnp.logical_and(i >= e * per, i < (e + 1) * per))
        def _(ext=ext):
            _dsa_prompt_step(q_ref, qi_ref, kwq_ref, o_ref, kt_ref, vb_ref, kit_ref, sel_ref,
                             tq=tq, topk=topk, s_len=ext)


def _dsa_prompt_step(q_ref, qi_ref, kwq_ref, o_ref, kt_ref, vb_ref, kit_ref, sel_ref, *, tq, topk, s_len):
    i = pl.program_id(1)
    kwq = kwq_ref[...]
    kit = kit_ref[:, :s_len]
    low = lax.broadcasted_iota(jnp.int32, (tq, LANES), 1) < IDX_DIM
    acc = jnp.zeros((tq, s_len), F32)
    for pair in range(IDX_HEADS // 2):
        qp = qi_ref[:, pair * LANES:(pair + 1) * LANES]
        for half in range(2):
            h = 2 * pair + half
            lhs = jnp.where(low if half == 0 else jnp.logical_not(low), qp, 0.0).astype(BF16)
            acc = acc + jnp.maximum(_dot(lhs, kit), 0.0) * kwq[:, IDX_DIM + h:IDX_DIM + h + 1]
    sc = acc * IDX_SCALE

    qpos = i * tq + lax.broadcasted_iota(jnp.int32, (tq, s_len), 0)
    kpos = lax.broadcasted_iota(jnp.int32, (tq, s_len), 1)
    _select_topk(sc, kpos <= qpos, topk, sel_ref.at[:, :s_len])
    sel = sel_ref[:, :s_len] > 0.5

    for kh in range(N_KV_HEADS):
        heads = [kh * GQA_GROUP + g for g in range(GQA_GROUP)]
        q4 = jnp.concatenate([q_ref[:, h * HEAD_DIM:(h + 1) * HEAD_DIM] for h in heads], axis=0).astype(BF16)
        s_all = _dot(q4, kt_ref[kh * HEAD_DIM:(kh + 1) * HEAD_DIM, :s_len])
        ps, ls = [], []
        for g in range(GQA_GROUP):
            sg = jnp.where(sel, s_all[g * tq:(g + 1) * tq, :] * ATTN_SCALE, -jnp.inf)
            p = jnp.exp(sg - jnp.max(sg, axis=-1, keepdims=True))
            ls.append(jnp.sum(p, axis=-1, keepdims=True))
            ps.append(p.astype(BF16))
        o = _dot(jnp.concatenate(ps, axis=0), vb_ref[:s_len, kh * HEAD_DIM:(kh + 1) * HEAD_DIM])
        for g, h in enumerate(heads):
            o_ref[:, h * HEAD_DIM:(h + 1) * HEAD_DIM] = (o[g * tq:(g + 1) * tq, :] / ls[g]).astype(o_ref.dtype)


def _dsa_prompt(p, *, nb, t, tq, topk):
    qw = N_KV_HEADS * GQA_GROUP * HEAD_DIM
    kvw = N_KV_HEADS * HEAD_DIM
    qiw = IDX_HEADS * IDX_DIM
    nq = t // tq
    k_blk, v_blk = qw // kvw, qw // kvw + 1
    qi_blk = (qw + 2 * kvw) // qiw
    kw_blk = (qw + 2 * kvw + qiw) // LANES
    assert qw % kvw == 0 and (qw + 2 * kvw) % qiw == 0 and (qw + 2 * kvw + qiw) % LANES == 0
    return pl.pallas_call(
        functools.partial(_dsa_prompt_body, tq=tq, topk=topk, n_ext=min(DSA_KEY_EXTENTS, nq)),
        grid=(nb, nq),
        in_specs=[
            pl.BlockSpec((tq, qw), lambda b, i: (b * nq + i, 0)),
            pl.BlockSpec((t, kvw), lambda b, i: (b, k_blk)),
            pl.BlockSpec((t, kvw), lambda b, i: (b, v_blk)),
            pl.BlockSpec((tq, qiw), lambda b, i: (b * nq + i, qi_blk)),
            pl.BlockSpec((tq, LANES), lambda b, i: (b * nq + i, kw_blk)),
            pl.BlockSpec((t, LANES), lambda b, i: (b, kw_blk)),
        ],
        out_specs=pl.BlockSpec((tq, qw), lambda b, i: (b * nq + i, 0)),
        out_shape=jax.ShapeDtypeStruct((nb * t, qw), BF16),
        scratch_shapes=[
            pltpu.VMEM((kvw, t), BF16),
            pltpu.VMEM((t, kvw), BF16),
            pltpu.VMEM((LANES, t), BF16),
            pltpu.VMEM((tq, t), F32),
        ],
        compiler_params=_params("parallel", "arbitrary"),
        name="dsa_prompt",
    )(p, p, p, p, p, p)


def _dsa_sample_scores_body(pt_ref, qi_ref, wi_ref, kin_ref, *rest, n_pages):
    del pt_ref
    page_refs, sc_ref = rest[:n_pages], rest[n_pages]
    qi = qi_ref[0].astype(BF16)
    new = jnp.broadcast_to(kin_ref[0], (IDX_DIM, PAGE_SIZE))
    keys_t = jnp.concatenate([r[0] for r in page_refs] + [new], axis=1).astype(BF16)
    s = jnp.maximum(_dot(qi, keys_t), 0.0) * wi_ref[0]
    sc_ref[0] = jnp.sum(s, axis=0, keepdims=True) * IDX_SCALE


def _dsa_sample_scores(page_table, qi, wi, ki_new, cache_ki_t, *, page0):
    n, n_pages = page_table.shape
    nkp = (n_pages + 1) * PAGE_SIZE
    page_specs = [pl.BlockSpec((1, IDX_DIM, PAGE_SIZE), lambda b, pt, pg=pg: (page0 + pt[b, pg], 0, 0))
                  for pg in range(n_pages)]
    return pl.pallas_call(
        functools.partial(_dsa_sample_scores_body, n_pages=n_pages),
        grid_spec=pltpu.PrefetchScalarGridSpec(
            num_scalar_prefetch=1,
            grid=(n,),
            in_specs=[
                pl.BlockSpec((1, IDX_HEADS, IDX_DIM), lambda b, pt: (b, 0, 0)),
                pl.BlockSpec((1, IDX_HEADS, 1), lambda b, pt: (b, 0, 0)),
                pl.BlockSpec((1, IDX_DIM, 1), lambda b, pt: (b, 0, 0)),
            ] + page_specs,
            out_specs=pl.BlockSpec((1, 1, nkp), lambda b, pt: (b, 0, 0)),
        ),
        out_shape=jax.ShapeDtypeStruct((n, 1, nkp), F32),
        compiler_params=_params("arbitrary"),
        name="dsa_sample_scores",
    )(page_table, qi, wi, ki_new, *([cache_ki_t] * n_pages))


def _dsa_sample_select_body(sc_ref, selt_ref, sel_ref, *, n_keys, topk):
    sc = sc_ref[...]
    kpos = lax.broadcasted_iota(jnp.int32, sc.shape, 1)
    _select_topk(sc, kpos < n_keys, topk, sel_ref)
    selt_ref[...] = sel_ref[...].T


def _dsa_sample_select(sc, *, n_keys, topk):
    n, nkp = sc.shape
    return pl.pallas_call(
        functools.partial(_dsa_sample_select_body, n_keys=n_keys, topk=topk),
        grid=(1,),
        in_specs=[pl.BlockSpec((n, nkp), lambda i: (0, 0))],
        out_specs=pl.BlockSpec((nkp, n), lambda i: (0, 0)),
        out_shape=jax.ShapeDtypeStruct((nkp, n), F32),
        scratch_shapes=[pltpu.VMEM((n, nkp), F32)],
        compiler_params=_params("arbitrary"),
        name="dsa_sample_select",
    )(sc)


def _row_onehot(h):
    return jnp.where(lax.broadcasted_iota(jnp.int32, (LANES, LANES), 0) == h, 1.0, 0.0).astype(BF16)


def _dsa_sample_attend_body(pt_ref, q_ref, kn_ref, vn_ref, selt_ref, *rest, n_pages):
    del pt_ref
    k_refs, v_refs = rest[:n_pages], rest[n_pages:2 * n_pages]
    o_ref, s_ref = rest[2 * n_pages], rest[2 * n_pages + 1]
    b = pl.program_id(0)
    n_heads = N_KV_HEADS * GQA_GROUP
    qpad = jnp.concatenate([q_ref[0], jnp.zeros((LANES - n_heads, HEAD_DIM), F32)], axis=0)
    row = lax.broadcasted_iota(jnp.int32, (LANES, HEAD_DIM), 0)
    q_kv = [jnp.where(row // GQA_GROUP == kv, qpad, 0.0).astype(BF16) for kv in range(N_KV_HEADS)]
    for pg in range(n_pages + 1):
        acc = None
        for kv in range(N_KV_HEADS):
            if pg < n_pages:
                kp = k_refs[pg][0, pl.ds(kv, PAGE_SIZE, stride=N_KV_HEADS), :]
            else:
                kp = jnp.broadcast_to(kn_ref[0, kv:kv + 1, :], (PAGE_SIZE, HEAD_DIM))
            part = _dot_nt(kp.astype(BF16), q_kv[kv])
            acc = part if acc is None else acc + part
        s_ref[pg * PAGE_SIZE:(pg + 1) * PAGE_SIZE, :] = acc
    selc = _dot(selt_ref[...].astype(BF16), _row_onehot(b))
    s = jnp.where(selc > 0.5, s_ref[...] * ATTN_SCALE, -jnp.inf)
    p = jnp.exp(s - jnp.max(s, axis=0, keepdims=True))
    pn = (p / jnp.sum(p, axis=0, keepdims=True)).astype(BF16)
    lane = lax.broadcasted_iota(jnp.int32, (HEAD_DIM, LANES), 1)
    o_t = jnp.zeros((HEAD_DIM, LANES), F32)
    for kv in range(N_KV_HEADS):
        acc = None
        for pg in range(n_pages + 1):
            if pg < n_pages:
                vp = v_refs[pg][0, pl.ds(kv, PAGE_SIZE, stride=N_KV_HEADS), :]
            else:
                vp = jnp.broadcast_to(vn_ref[0, kv:kv + 1, :], (PAGE_SIZE, HEAD_DIM))
            part = _dot_tn(vp.astype(BF16), pn[pg * PAGE_SIZE:(pg + 1) * PAGE_SIZE, :])
            acc = part if acc is None else acc + part
        o_t = jnp.where(lane // GQA_GROUP == kv, acc, o_t)
    o_ref[0] = o_t.T[:n_heads, :]


def _dsa_sample_attend(page_table, q, k_new, v_new, selt, cache_k, cache_v, *, page0):
    n, n_pages = page_table.shape
    n_heads = N_KV_HEADS * GQA_GROUP
    nkp = (n_pages + 1) * PAGE_SIZE
    assert n == LANES, "the selection column is picked with a one-hot over the lane axis"
    page_specs = [pl.BlockSpec((1, PAGE_SIZE * N_KV_HEADS, HEAD_DIM),
                               lambda b, pt, pg=pg: (page0 + pt[b, pg], 0, 0)) for pg in range(n_pages)]
    return pl.pallas_call(
        functools.partial(_dsa_sample_attend_body, n_pages=n_pages),
        grid_spec=pltpu.PrefetchScalarGridSpec(
            num_scalar_prefetch=1,
            grid=(n,),
            in_specs=[
                pl.BlockSpec((1, n_heads, HEAD_DIM), lambda b, pt: (b, 0, 0)),
                pl.BlockSpec((1, N_KV_HEADS, HEAD_DIM), lambda b, pt: (b, 0, 0)),
                pl.BlockSpec((1, N_KV_HEADS, HEAD_DIM), lambda b, pt: (b, 0, 0)),
                pl.BlockSpec((nkp, n), lambda b, pt: (0, 0)),
            ] + page_specs + page_specs,
            out_specs=pl.BlockSpec((1, n_heads, HEAD_DIM), lambda b, pt: (b, 0, 0)),
            scratch_shapes=[pltpu.VMEM((nkp, LANES), F32)],
        ),
        out_shape=jax.ShapeDtypeStruct((n, n_heads, HEAD_DIM), F32),
        compiler_params=_params("arbitrary"),
        name="dsa_sample_attend",
    )(page_table, q, k_new, v_new, selt, *([cache_k] * n_pages), *([cache_v] * n_pages))


def _memx_prompt_body(q_ref, mk_ref, mv_ref, o_ref):
    for h in range(MEM_HEADS):
        cs = slice(h * MEM_HEAD_DIM, (h + 1) * MEM_HEAD_DIM)
        s = _dot_nt(q_ref[:, cs].astype(BF16), mk_ref[:, cs].astype(BF16)) * MEM_SCALE
        p = jnp.exp(s - jnp.max(s, axis=-1, keepdims=True))
        pn = (p / jnp.sum(p, axis=-1, keepdims=True)).astype(BF16)
        o_ref[:, cs] = _dot(pn, mv_ref[:, cs].astype(BF16)).astype(o_ref.dtype)


def _memx_prompt(q, mk, mv, *, nb, t, tq):
    w = q.shape[1]
    m_len = mk.shape[0] // nb
    nq = t // tq
    return pl.pallas_call(
        _memx_prompt_body,
        grid=(nb, nq),
        in_specs=[
            pl.BlockSpec((tq, w), lambda b, i: (b * nq + i, 0)),
            pl.BlockSpec((m_len, w), lambda b, i: (b, 0)),
            pl.BlockSpec((m_len, w), lambda b, i: (b, 0)),
        ],
        out_specs=pl.BlockSpec((tq, w), lambda b, i: (b * nq + i, 0)),
        out_shape=jax.ShapeDtypeStruct((nb * t, w), BF16),
        compiler_params=_params("parallel", "arbitrary"),
        name="memx_prompt",
    )(q, mk, mv)


def _memx_sample_body(q_ref, mk_ref, mv_ref, o_ref, *, ns):
    m_len = mk_ref.shape[1] // MEM_HEADS
    row = lax.broadcasted_iota(jnp.int32, (LANES, MEM_HEAD_DIM), 0)
    spread = [_row_onehot(h) for h in range(MEM_HEADS)]
    for i in range(ns):
        s = None
        for h in range(MEM_HEADS):
            qh = jnp.where(row == h, jnp.broadcast_to(q_ref[i, h:h + 1, :], (LANES, MEM_HEAD_DIM)), 0.0)
            part = _dot_nt(mk_ref[i, pl.ds(h, m_len, stride=MEM_HEADS), :].astype(BF16), qh.astype(BF16))
            s = part if s is None else s + part
        s = s * MEM_SCALE
        p = jnp.exp(s - jnp.max(s, axis=0, keepdims=True))
        pn = (p / jnp.sum(p, axis=0, keepdims=True)).astype(BF16)
        outs = [jnp.sum(_dot(pn, spread[h]) * mv_ref[i, pl.ds(h, m_len, stride=MEM_HEADS), :], axis=0, keepdims=True)
                for h in range(MEM_HEADS)]
        o_ref[i] = jnp.concatenate(outs, axis=0)


def _memx_sample(q, mk, mv, *, seq0, ns):
    n = q.shape[0]
    rows = mk.shape[1]
    blk0 = seq0 // ns
    assert seq0 % ns == 0
    return pl.pallas_call(
        functools.partial(_memx_sample_body, ns=ns),
        grid=(n // ns,),
        in_specs=[
            pl.BlockSpec((ns, MEM_HEADS, MEM_HEAD_DIM), lambda i: (i, 0, 0)),
            pl.BlockSpec((ns, rows, MEM_HEAD_DIM), lambda i: (blk0 + i, 0, 0)),
            pl.BlockSpec((ns, rows, MEM_HEAD_DIM), lambda i: (blk0 + i, 0, 0)),
        ],
        out_specs=pl.BlockSpec((ns, MEM_HEADS, MEM_HEAD_DIM), lambda i: (i, 0, 0)),
        out_shape=jax.ShapeDtypeStruct((n, MEM_HEADS, MEM_HEAD_DIM), F32),
        compiler_params=_params("parallel"),
        name="memx_sample",
    )(q, mk, mv)


def _row_tile(m, cap):
    t = min(m, cap)
    assert m % t == 0
    return t


def _col_tile(n, cap):
    best = LANES
    for t in range(LANES, min(n, cap) + 1, LANES):
        if n % t == 0:
            best = t
    assert n % best == 0
    return best


def _proj(x, w):
    return _mm(x, w, tm=_row_tile(x.shape[0], 1024), tn=_col_tile(w.shape[1], 512))


def kernel(x_prompt, x_sample, cache_attn_k, cache_attn_v, cache_idx_k, state_conv, cache_mem_k, cache_mem_v, page_table, mem_prompt, w_in_mix, sgu_w, sgu_b, sgu_ln_g, sgu_ln_b, conv_w, w_out_mix, w_in_attn, w_out_attn, w_mem_q, w_mem_k, w_mem_v, w_mem_o, ffn_w_gu, ffn_w_down, ln_g, ln_b):
    nb, t, d = x_prompt.shape
    ns = x_sample.shape[0]
    assert x_sample.shape[1] == 1
    depth = ffn_w_gu.shape[0]
    d_ff = ffn_w_down.shape[2]
    m_len = mem_prompt.shape[1]
    mem_w = w_mem_q.shape[2]
    n_pages = page_table.shape[1]
    past = n_pages * PAGE_SIZE
    aw = sgu_ln_g.shape[1] * A_GROUP
    bw = conv_w.shape[2]
    qw = N_KV_HEADS * GQA_GROUP * HEAD_DIM
    kvw = N_KV_HEADS * HEAD_DIM
    qiw = IDX_HEADS * IDX_DIM

    hp = x_prompt.reshape(nb * t, d)
    hs = x_sample.reshape(ns, d)
    mem2 = mem_prompt.reshape(nb * m_len, d)
    tm_p = _row_tile(nb * t, 512)
    tf = _col_tile(d_ff, 512)

    def ln_params(l, i):
        return ln_g[l, i].reshape(1, d), ln_b[l, i].reshape(1, d)

    w_gu_bf = ffn_w_gu.astype(BF16)
    w_down_bf = ffn_w_down.astype(BF16)

    def ffn(h, l, i, tm):
        g, b = ln_params(l, 3 * i)
        return _ffn_ln(h, w_gu_bf, w_down_bf, g, b, layer=l, slot=i, tm=tm, tf=tf)

    n_pool = cache_idx_k.shape[1]
    cache_k = cache_attn_k.reshape(-1, PAGE_SIZE * N_KV_HEADS, HEAD_DIM)
    cache_v = cache_attn_v.reshape(-1, PAGE_SIZE * N_KV_HEADS, HEAD_DIM)
    cache_ki_t = jnp.swapaxes(cache_idx_k, 2, 3).reshape(-1, IDX_DIM, PAGE_SIZE)
    cache_mk = cache_mem_k.reshape(-1, m_len * MEM_HEADS, MEM_HEAD_DIM)
    cache_mv = cache_mem_v.reshape(-1, m_len * MEM_HEADS, MEM_HEAD_DIM)

    attn_k_p, attn_v_p, idx_k_p, conv_p, mem_k_p, mem_v_p = [], [], [], [], [], []
    attn_k_s, attn_v_s, idx_k_s, conv_s, sgu_v_s = [], [], [], [], []
    for l in range(depth):
        j = l // 2
        hp = ffn(hp, l, 0, tm_p)
        hs = ffn(hs, l, 0, ns)

        g1, b1 = ln_params(l, 1)
        if l % 2 == 0:
            w_in = w_in_mix[j].astype(BF16)
            w_out = w_out_mix[j].astype(BF16)
            taps = conv_w[j]
            pp = _proj(hp, w_in)
            zp, ctail = _mix_prompt(pp, sgu_w[j], sgu_b[j].T, sgu_ln_g[j], sgu_ln_b[j], taps,
                                    nb=nb, t=t, tt=_row_tile(t, 256))
            conv_p.append(ctail[:, SUBLANES - (CONV_W - 1):, :])
            hp = _mm_res_ln(zp, w_out, hp, g1, b1, tm=tm_p)
            ps = _proj(hs, w_in)
            w0 = jnp.repeat(sgu_w[j][:, 0, 0], A_GROUP).reshape(1, aw)
            b0 = jnp.repeat(sgu_b[j][:, 0], A_GROUP).reshape(1, aw)
            zs, vn_s, cin_s = _mix_sample(ps, state_conv[j][:, 0, :], state_conv[j][:, 1, :], w0, b0,
                                          sgu_ln_g[j].reshape(1, aw), sgu_ln_b[j].reshape(1, aw), taps)
            conv_s.append(jnp.stack([state_conv[j][:, 1, :], cin_s], axis=1))
            sgu_v_s.append(vn_s.reshape(ns, 1, aw // A_GROUP, A_GROUP))
            hs = _mm_res_ln(zs, w_out, hs, g1, b1, tm=ns)
        else:
            n_in = w_in_attn.shape[2]
            n_pad = -n_in % LANES
            w_in = jnp.pad(w_in_attn[j], ((0, 0), (0, n_pad))).astype(BF16)
            w_out = w_out_attn[j].astype(BF16)
            pp = _proj(hp, w_in)
            op = _dsa_prompt(pp, nb=nb, t=t, tq=QBLOCK_ROWS, topk=min(TOPK_MAX, t // 4))
            attn_k_p.append(pp[:, qw:qw + kvw].reshape(nb, t, N_KV_HEADS, HEAD_DIM))
            attn_v_p.append(pp[:, qw + kvw:qw + 2 * kvw].reshape(nb, t, N_KV_HEADS, HEAD_DIM))
            idx_k_p.append(pp[:, qw + 2 * kvw + qiw:qw + 2 * kvw + qiw + IDX_DIM].reshape(nb, t, IDX_DIM))
            hp = _mm_res_ln(op, w_out, hp, g1, b1, tm=tm_p)
            ps = _proj(hs, w_in)
            q_s = ps[:, :qw]
            k_s = ps[:, qw:qw + kvw]
            v_s = ps[:, qw + kvw:qw + 2 * kvw]
            qi_s = ps[:, qw + 2 * kvw:qw + 2 * kvw + qiw].reshape(ns, IDX_HEADS, IDX_DIM)
            ki_s = ps[:, qw + 2 * kvw + qiw:qw + 2 * kvw + qiw + IDX_DIM]
            wi_s = ps[:, qw + 2 * kvw + qiw + IDX_DIM:qw + 2 * kvw + qiw + IDX_DIM + IDX_HEADS]
            sc = _dsa_sample_scores(page_table, qi_s, wi_s.reshape(ns, IDX_HEADS, 1), ki_s.reshape(ns, IDX_DIM, 1),
                                    cache_ki_t, page0=j * n_pool)
            selt = _dsa_sample_select(sc.reshape(ns, -1), n_keys=past + 1, topk=min(TOPK_MAX, (past + 1) // 4))
            os_ = _dsa_sample_attend(page_table, q_s.reshape(ns, N_KV_HEADS * GQA_GROUP, HEAD_DIM),
                                     k_s.reshape(ns, N_KV_HEADS, HEAD_DIM), v_s.reshape(ns, N_KV_HEADS, HEAD_DIM),
                                     selt, cache_k, cache_v, page0=j * n_pool).reshape(ns, qw)
            attn_k_s.append(k_s.reshape(ns, 1, N_KV_HEADS, HEAD_DIM))
            attn_v_s.append(v_s.reshape(ns, 1, N_KV_HEADS, HEAD_DIM))
            idx_k_s.append(ki_s.reshape(ns, 1, IDX_DIM))
            hs = _mm_res_ln(os_, w_out, hs, g1, b1, tm=ns)

        g2, b2 = ln_params(l, 2)
        w_q = w_mem_q[l].astype(BF16)
        w_o = w_mem_o[l].astype(BF16)
        mk = _proj(mem2, w_mem_k[l].astype(BF16))
        mv = _proj(mem2, w_mem_v[l].astype(BF16))
        mem_k_p.append(mk.reshape(nb, m_len, MEM_HEADS, MEM_HEAD_DIM))
        mem_v_p.append(mv.reshape(nb, m_len, MEM_HEADS, MEM_HEAD_DIM))
        op = _memx_prompt(_proj(hp, w_q), mk, mv, nb=nb, t=t, tq=_row_tile(t, 512))
        hp = _mm_res_ln(op, w_o, hp, g2, b2, tm=tm_p)
        os_ = _memx_sample(_proj(hs, w_q).reshape(ns, MEM_HEADS, MEM_HEAD_DIM), cache_mk, cache_mv,
                           seq0=l * ns, ns=_row_tile(ns, SUBLANES))
        hs = _mm_res_ln(os_.reshape(ns, mem_w), w_o, hs, g2, b2, tm=ns)

        hp = ffn(hp, l, 1, tm_p)
        hs = ffn(hs, l, 1, ns)

    return (hp.reshape(nb, t, d), hs.reshape(ns, 1, d),
            jnp.stack(attn_k_p), jnp.stack(attn_v_p), jnp.stack(idx_k_p), jnp.stack(conv_p),
            jnp.stack(mem_k_p), jnp.stack(mem_v_p),
            jnp.stack(attn_k_s), jnp.stack(attn_v_s), jnp.stack(idx_k_s), jnp.stack(conv_s), jnp.stack(sgu_v_s))
```
